```python
import math
import jax, jax.numpy as jnp
from jax import lax
import numpy as np

D_MODEL = 4096
BATCH = 1
SEQ = 8192
DEPTH = 2

CHUNK = 64

MIX_HALF = D_MODEL // 2
HEAD_DIM = 128
N_ATTN_HEADS = MIX_HALF // HEAD_DIM
Q_BLOCK = 128
FORGET_BIAS = 3.0
POOL_WINDOWS = (2, 4, 8, 16)
POOL_GROUPS = len(POOL_WINDOWS)
POOL_GROUP_DIM = MIX_HALF // POOL_GROUPS
CONV_WIDTH = MIX_HALF
CONV_K = 3
SGU_WIDTH = MIX_HALF
SGU_CHUNK = 128
SGU_HEADS = 16
SGU_HEAD_DIM = SGU_WIDTH // SGU_HEADS
D_FF = ((8 * D_MODEL // 3 + 255) // 256) * 256
FFN_CONV_K = 3
EPS = 1e-6

EVEN_IN = 3 * MIX_HALF + N_ATTN_HEADS + MIX_HALF
ODD_IN = 3 * CONV_WIDTH + 2 * SGU_WIDTH

kernel_name = "hybrid_fox_pool_conv_sgu_trunk"


def rmsnorm(x, g):
    xf = x.astype(jnp.float32)
    y = xf * lax.rsqrt(jnp.mean(xf * xf, axis=-1, keepdims=True) + EPS)
    return (y * g.astype(jnp.float32)).astype(x.dtype)


def causal_dwconv(x, w):
    k = w.shape[0]
    s = x.shape[1]
    xp = jnp.pad(x, ((0, 0), (k - 1, 0), (0, 0)))
    y = w[0] * xp[:, 0:s]
    for i in range(1, k):
        y = y + w[i] * xp[:, i:i + s]
    return y


def forgetting_attention(q, k, v, f_logit, b_f, q_gain, k_gain):
    b, s, h, d = q.shape
    log_f = jax.nn.log_sigmoid(f_logit.astype(jnp.float32) + b_f.astype(jnp.float32))
    cum = jnp.cumsum(log_f, axis=1).transpose(0, 2, 1)
    q = rmsnorm(q, q_gain)
    k = rmsnorm(k, k_gain)
    scale = 1.0 / math.sqrt(d)
    nblk = s // Q_BLOCK
    q_blocks = q.reshape(b, nblk, Q_BLOCK, h, d).transpose(1, 0, 2, 3, 4)
    cum_blocks = cum.reshape(b, h, nblk, Q_BLOCK).transpose(2, 0, 1, 3)
    kpos = jnp.arange(s)

    def one_block(args):
        qb, cq, i = args
        qpos = i * Q_BLOCK + jnp.arange(Q_BLOCK)
        logits = jnp.einsum('bqhd,bkhd->bhqk', qb, k,
                            preferred_element_type=jnp.float32) * scale
        logits = logits + (cq[:, :, :, None] - cum[:, :, None, :])
        logits = jnp.where(kpos[None, :] <= qpos[:, None], logits, -1e30)
        p = jax.nn.softmax(logits, axis=-1)
        return jnp.einsum('bhqk,bkhd->bqhd', p.astype(v.dtype), v)

    out = lax.map(one_block, (q_blocks, cum_blocks, jnp.arange(nblk)))
    return out.transpose(1, 0, 2, 3, 4).reshape(b, s, h * d)


def multiscale_pool(z, pool_w, pool_scale):
    b, s, _ = z.shape
    zg = z.reshape(b, s, POOL_GROUPS, POOL_GROUP_DIM)
    cs = jnp.cumsum(zg.astype(jnp.float32), axis=1)
    pos = jnp.arange(1, s + 1, dtype=jnp.float32)
    outs = []
    for g, w in enumerate(POOL_WINDOWS):
        c = cs[:, :, g]
        c_prev = jnp.pad(c, ((0, 0), (w, 0), (0, 0)))[:, :s]
        cnt = jnp.minimum(pos, float(w))[None, :, None]
        pooled = (c - c_prev) / cnt - zg[:, :, g].astype(jnp.float32)
        outs.append(jnp.einsum('bsc,cd->bsd', pooled.astype(z.dtype), pool_w[g]))
    return jnp.concatenate(outs, axis=-1) * pool_scale


def short_gated_conv(v, gate_b, gate_c, conv_w):
    return gate_b * causal_dwconv(gate_c * v, conv_w)


def spatial_gating(z, norm_g, w_s, b_s):
    z = jax.nn.gelu(z, approximate=False)
    u, v = jnp.split(z, 2, axis=-1)
    v = rmsnorm(v, norm_g)
    b, s, _ = v.shape
    n = s // SGU_CHUNK
    vc = v.reshape(b, n, SGU_CHUNK, SGU_HEADS, SGU_HEAD_DIM)
    w = jnp.tril(w_s)
    mixed = jnp.einsum('hts,bnshc->bnthc', w, vc) + b_s.T[None, None, :, :, None]
    return u * mixed.reshape(b, s, SGU_WIDTH)


def even_mixer(h, w_in, b_f, q_gain, k_gain, pool_w, pool_scale, w_out):
    b, s, _ = h.shape
    p = h @ w_in
    q, k, v, f_logit, zp = jnp.split(
        p, [MIX_HALF, 2 * MIX_HALF, 3 * MIX_HALF, 3 * MIX_HALF + N_ATTN_HEADS], axis=-1)
    shp = (b, s, N_ATTN_HEADS, HEAD_DIM)
    a = forgetting_attention(q.reshape(shp), k.reshape(shp), v.reshape(shp),
                             f_logit, b_f, q_gain, k_gain)
    pm = multiscale_pool(zp, pool_w, pool_scale)
    return jnp.concatenate([a, pm], axis=-1) @ w_out


def odd_mixer(h, w_in, conv_w, sgu_norm, sgu_w, sgu_b, w_out):
    p = h @ w_in
    v, gate_b, gate_c, zs = jnp.split(p, [CONV_WIDTH, 2 * CONV_WIDTH, 3 * CONV_WIDTH], axis=-1)
    c_out = short_gated_conv(v, gate_b, gate_c, conv_w)
    d_out = spatial_gating(zs, sgu_norm, sgu_w, sgu_b)
    return jnp.concatenate([c_out, d_out], axis=-1) @ w_out


def conv_glu_ffn(h, w_up, conv_w, w_down):
    g, u = jnp.split(h @ w_up, 2, axis=-1)
    g = causal_dwconv(g, conv_w)
    return (jax.nn.silu(g) * u) @ w_down


def setup_inputs(seed: int = 0) -> dict:
    key = jax.random.key(seed)
    ks = jax.random.split(key, 24)
    f32 = jnp.float32

    def nrm(k, shape, scale):
        return jax.random.normal(k, shape, f32) * scale

    def gain(k, n):
        return 1.0 + 0.05 * jax.random.normal(k, (n,), f32)

    return {
        'x': jax.random.normal(ks[0], (BATCH, SEQ, D_MODEL), f32),
        'l0_mix_norm': gain(ks[1], D_MODEL),
        'l0_w_in': nrm(ks[2], (D_MODEL, EVEN_IN), D_MODEL ** -0.5),
        'l0_b_f': FORGET_BIAS + 0.5 * jax.random.normal(ks[3], (N_ATTN_HEADS,), f32),
        'l0_q_gain': gain(ks[4], HEAD_DIM),
        'l0_k_gain': gain(ks[5], HEAD_DIM),
        'l0_pool_w': nrm(ks[6], (POOL_GROUPS, POOL_GROUP_DIM, POOL_GROUP_DIM), POOL_GROUP_DIM ** -0.5),
        'l0_pool_scale': 1.0 + 0.1 * jax.random.normal(ks[7], (MIX_HALF,), f32),
        'l0_w_out': nrm(ks[8], (2 * MIX_HALF, D_MODEL), (2 * MIX_HALF) ** -0.5),
        'l0_ffn_norm': gain(ks[9], D_MODEL),
        'l0_ffn_w_up': nrm(ks[10], (D_MODEL, 2 * D_FF), D_MODEL ** -0.5),
        'l0_ffn_conv': nrm(ks[11], (FFN_CONV_K, D_FF), FFN_CONV_K ** -0.5),
        'l0_ffn_w_down': nrm(ks[12], (D_FF, D_MODEL), D_FF ** -0.5),
        'l1_mix_norm': gain(ks[13], D_MODEL),
        'l1_w_in': nrm(ks[14], (D_MODEL, ODD_IN), D_MODEL ** -0.5),
        'l1_conv_w': nrm(ks[15], (CONV_K, CONV_WIDTH), CONV_K ** -0.5),
        'l1_sgu_norm': gain(ks[16], SGU_WIDTH),
        'l1_sgu_w': nrm(ks[17], (SGU_HEADS, SGU_CHUNK, SGU_CHUNK), SGU_CHUNK ** -0.5),
        'l1_sgu_b': 1.0 + 0.1 * jax.random.normal(ks[18], (SGU_HEADS, SGU_CHUNK), f32),
        'l1_w_out': nrm(ks[19], (CONV_WIDTH + SGU_WIDTH, D_MODEL), (CONV_WIDTH + SGU_WIDTH) ** -0.5),
        'l1_ffn_norm': gain(ks[20], D_MODEL),
        'l1_ffn_w_up': nrm(ks[21], (D_MODEL, 2 * D_FF), D_MODEL ** -0.5),
        'l1_ffn_conv': nrm(ks[22], (FFN_CONV_K, D_FF), FFN_CONV_K ** -0.5),
        'l1_ffn_w_down': nrm(ks[23], (D_FF, D_MODEL), D_FF ** -0.5),
    }


def reference(x, l0_mix_norm, l0_w_in, l0_b_f, l0_q_gain, l0_k_gain, l0_pool_w,
              l0_pool_scale, l0_w_out, l0_ffn_norm, l0_ffn_w_up, l0_ffn_conv, l0_ffn_w_down,
              l1_mix_norm, l1_w_in, l1_conv_w, l1_sgu_norm, l1_sgu_w, l1_sgu_b, l1_w_out,
              l1_ffn_norm, l1_ffn_w_up, l1_ffn_conv, l1_ffn_w_down):
    mix_norms = (l0_mix_norm, l1_mix_norm)
    even_params = (l0_w_in, l0_b_f, l0_q_gain, l0_k_gain, l0_pool_w, l0_pool_scale, l0_w_out)
    odd_params = (l1_w_in, l1_conv_w, l1_sgu_norm, l1_sgu_w, l1_sgu_b, l1_w_out)
    ffn_params = ((l0_ffn_norm, l0_ffn_w_up, l0_ffn_conv, l0_ffn_w_down),
                  (l1_ffn_norm, l1_ffn_w_up, l1_ffn_conv, l1_ffn_w_down))
    for layer in range(DEPTH):
        h = rmsnorm(x, mix_norms[layer])
        if layer % 2 == 0:
            x = x + even_mixer(h, *even_params)
        else:
            x = x + odd_mixer(h, *odd_params)
        f_norm, f_up, f_conv, f_down = ffn_params[layer]
        x = x + conv_glu_ffn(rmsnorm(x, f_norm), f_up, f_conv, f_down)
    return x
```

```python
import functools
import math

import jax
import jax.numpy as jnp
from jax import lax
from jax.experimental import pallas as pl
from jax.experimental.pallas import tpu as pltpu

F32 = jnp.float32
BF16 = jnp.bfloat16

EPS = 1e-6
HEAD_DIM = 128
POOL_WINDOWS = (2, 4, 8, 16)
SGU_CHUNK = 128
NEG_BIG = -1e30

V7X_VMEM_BYTES = 64 << 20
VMEM_CAP = V7X_VMEM_BYTES - (6 << 20)
SUBLANES = 8


def _params(sem, vmem_bytes):
    return pltpu.CompilerParams(
        dimension_semantics=sem,
        vmem_limit_bytes=int(min(max(vmem_bytes, 16 << 20), VMEM_CAP)),
    )


def _single(block, index_map):
    return pl.BlockSpec(block, index_map, pipeline_mode=pl.Buffered(1))


def _rmsnorm_body(x_ref, g_ref, o_ref):
    x = x_ref[...]
    ms = jnp.mean(x * x, axis=-1, keepdims=True)
    o_ref[...] = ((x * lax.rsqrt(ms + EPS)) * g_ref[...]).astype(o_ref.dtype)


def rmsnorm_rows(x, g, *, tr=256):
    s, d = x.shape
    return pl.pallas_call(
        _rmsnorm_body,
        grid=(s // tr,),
        in_specs=[pl.BlockSpec((tr, d), lambda i: (i, 0)),
                  pl.BlockSpec((1, d), lambda i: (0, 0))],
        out_specs=pl.BlockSpec((tr, d), lambda i: (i, 0)),
        out_shape=jax.ShapeDtypeStruct((s, d), BF16),
        compiler_params=_params(("arbitrary",), 2 * tr * d * 4 + 2 * tr * d * 2 + 4 * tr * d * 4),
        name="rmsnorm",
    )(x, g.reshape(1, d))


def _qkv_body(a_ref, w_ref, g_ref, o_ref, *, n_norm_tiles, heads_per_tile):
    j = pl.program_id(1)
    acc = jnp.dot(a_ref[...], w_ref[...], preferred_element_type=F32)

    @pl.when(j < n_norm_tiles)
    def _():
        for hh in range(heads_per_tile):
            sl = slice(hh * HEAD_DIM, (hh + 1) * HEAD_DIM)
            blk = acc[:, sl]
            ms = jnp.mean(blk * blk, axis=-1, keepdims=True)
            o_ref[:, sl] = ((blk * lax.rsqrt(ms + EPS)) * g_ref[:, sl]).astype(o_ref.dtype)

    @pl.when(j >= n_norm_tiles)
    def _():
        o_ref[...] = acc.astype(o_ref.dtype)


def qkv_proj(h, w_qkv, qk_gain, *, tm=1024, tn=512):
    s, kdim = h.shape
    n = w_qkv.shape[1]
    n_qk = qk_gain.shape[1]
    n_norm_tiles = n_qk // tn
    body = functools.partial(_qkv_body, n_norm_tiles=n_norm_tiles, heads_per_tile=tn // HEAD_DIM)
    vmem = tm * kdim * 2 + 2 * kdim * tn * 2 + 2 * tm * tn * 2 + 4 * tm * tn * 4
    return pl.pallas_call(
        body,
        grid=(s // tm, n // tn),
        in_specs=[_single((tm, kdim), lambda i, j: (i, 0)),
                  pl.BlockSpec((kdim, tn), lambda i, j: (0, j)),
                  pl.BlockSpec((1, tn), lambda i, j: (0, jnp.minimum(j, n_norm_tiles - 1)))],
        out_specs=pl.BlockSpec((tm, tn), lambda i, j: (i, j)),
        out_shape=jax.ShapeDtypeStruct((s, n), BF16),
        compiler_params=_params(("arbitrary", "arbitrary"), vmem),
        name="qkv_proj",
    )(h, w_qkv, qk_gain)


def _mm_body(a_ref, w_ref, o_ref):
    o_ref[...] = jnp.dot(a_ref[...], w_ref[...], preferred_element_type=F32).astype(o_ref.dtype)


def matmul(a, w, *, out_dtype, tm=1024, tn=512, col_block_offset=0, n_out=None):
    s, kdim = a.shape
    n = w.shape[1] if n_out is None else n_out
    osz = jnp.dtype(out_dtype).itemsize
    vmem = tm * kdim * 2 + 2 * kdim * tn * 2 + 2 * tm * tn * osz + 2 * tm * tn * 4
    return pl.pallas_call(
        _mm_body,
        grid=(s // tm, n // tn),
        in_specs=[_single((tm, kdim), lambda i, j: (i, 0)),
                  pl.BlockSpec((kdim, tn), lambda i, j: (0, j + col_block_offset))],
        out_specs=pl.BlockSpec((tm, tn), lambda i, j: (i, j)),
        out_shape=jax.ShapeDtypeStruct((s, n), out_dtype),
        compiler_params=_params(("arbitrary", "arbitrary"), vmem),
        name="matmul",
    )(a, w)


def _cumsum_rows(x):
    n = x.shape[0]
    row = lax.broadcasted_iota(jnp.int32, x.shape, 0)
    d = 1
    while d < n:
        x = x + jnp.where(row >= d, pltpu.roll(x, d, axis=0), 0.0)
        d *= 2
    return x


def _gate_body(a_ref, w_ref, b_ref, cum_ref, cumt_ref, carry_ref, *, n_heads):
    i = pl.program_id(0)

    @pl.when(i == 0)
    def _():
        carry_ref[...] = jnp.zeros_like(carry_ref)

    f = jnp.dot(a_ref[...], w_ref[...], preferred_element_type=F32) + b_ref[...]
    log_f = jnp.minimum(f, 0.0) - jnp.log1p(jnp.exp(-jnp.abs(f)))
    cum = _cumsum_rows(log_f) + carry_ref[...]
    cum_ref[...] = cum
    cumt_ref[...] = jnp.transpose(cum)[:n_heads, :]
    carry_ref[...] = cum[-1:, :]


def forget_gate(h, w_f, b_f, n_heads, *, tg=1024):
    s, kdim = h.shape
    lanes = w_f.shape[1]
    body = functools.partial(_gate_body, n_heads=n_heads)
    vmem = 2 * tg * kdim * 2 + 2 * kdim * lanes * 2 + 16 * tg * lanes * 4
    return pl.pallas_call(
        body,
        grid=(s // tg,),
        in_specs=[pl.BlockSpec((tg, kdim), lambda i: (i, 0)),
                  pl.BlockSpec((kdim, lanes), lambda i: (0, 0)),
                  pl.BlockSpec((1, lanes), lambda i: (0, 0))],
        out_specs=[pl.BlockSpec((tg, lanes), lambda i: (i, 0)),
                   pl.BlockSpec((n_heads, tg), lambda i: (0, i))],
        out_shape=[jax.ShapeDtypeStruct((s, lanes), F32),
                   jax.ShapeDtypeStruct((n_heads, s), F32)],
        scratch_shapes=[pltpu.VMEM((1, lanes), F32)],
        compiler_params=_params(("arbitrary",), vmem),
        name="forget_gate",
    )(h, w_f, b_f)


def _attn_body(q_ref, k_ref, v_ref, cq_ref, ck_ref, o_ref, m_ref, l_ref, acc_ref, *, tq, tk, scale):
    h = pl.program_id(0)
    qb = pl.program_id(1)
    q = q_ref[...]
    lane = lax.broadcasted_iota(jnp.int32, (tq, HEAD_DIM), 1)
    cq = jnp.sum(jnp.where(lane == h, cq_ref[...], 0.0), axis=1, keepdims=True)

    m_ref[...] = jnp.full_like(m_ref, NEG_BIG)
    l_ref[...] = jnp.zeros_like(l_ref)
    acc_ref[...] = jnp.zeros_like(acc_ref)

    def block(kb, masked):
        start = pl.multiple_of(kb * tk, tk)
        k = k_ref[pl.ds(start, tk), :]
        v = v_ref[pl.ds(start, tk), :]
        s = lax.dot_general(q, k, (((1,), (1,)), ((), ())), preferred_element_type=F32)
        s = s * scale + (cq - ck_ref[0, kb])
        if masked:
            row = lax.broadcasted_iota(jnp.int32, (tq, tk), 0)
            col = lax.broadcasted_iota(jnp.int32, (tq, tk), 1)
            s = jnp.where(col <= row, s, NEG_BIG)
        m_old = m_ref[...]
        m_new = jnp.maximum(m_old, jnp.max(s, axis=1, keepdims=True))
        alpha = jnp.exp(m_old - m_new)
        p = jnp.exp(s - m_new)
        l_ref[...] = alpha * l_ref[...] + jnp.sum(p, axis=1, keepdims=True)
        acc_ref[...] = alpha * acc_ref[...] + jnp.dot(p.astype(BF16), v, preferred_element_type=F32)
        m_ref[...] = m_new

    def full_block(kb, carry):
        block(kb, False)
        return carry

    lax.fori_loop(0, qb, full_block, 0)
    block(qb, True)
    o_ref[...] = (acc_ref[...] / l_ref[...]).astype(o_ref.dtype)


def fox_attention(qkv, cum, cum_t, n_heads, *, tq=512):
    s = qkv.shape[0]
    tk = tq
    nk = s // tk
    ck = cum_t.reshape(n_heads, nk, 1, tk)
    body = functools.partial(_attn_body, tq=tq, tk=tk, scale=1.0 / math.sqrt(HEAD_DIM))
    vmem = (4 * s * HEAD_DIM * 2 + 2 * tq * HEAD_DIM * 4 + 2 * s * 4 + 8 * tq * HEAD_DIM * 4
            + 6 * tq * tk * 4)
    return pl.pallas_call(
        body,
        grid=(n_heads, s // tq),
        in_specs=[pl.BlockSpec((tq, HEAD_DIM), lambda h, i: (i, h)),
                  pl.BlockSpec((s, HEAD_DIM), lambda h, i: (0, n_heads + h)),
                  pl.BlockSpec((s, HEAD_DIM), lambda h, i: (0, 2 * n_heads + h)),
                  pl.BlockSpec((tq, HEAD_DIM), lambda h, i: (i, 0)),
                  pl.BlockSpec((1, nk, 1, tk), lambda h, i: (h, 0, 0, 0))],
        out_specs=pl.BlockSpec((tq, HEAD_DIM), lambda h, i: (i, h)),
        out_shape=jax.ShapeDtypeStruct((s, n_heads * HEAD_DIM), BF16),
        scratch_shapes=[pltpu.VMEM((tq, 1), F32), pltpu.VMEM((tq, 1), F32),
                        pltpu.VMEM((tq, HEAD_DIM), F32)],
        compiler_params=_params(("arbitrary", "arbitrary"), vmem),
        name="fox_attention",
    )(qkv, qkv, qkv, cum, ck)


POOL_HALO = 16


def _pool_body(z_ref, halo_ref, w_ref, sc_ref, o_ref, *, tp):
    g = pl.program_id(0)
    i = pl.program_id(1)
    z = z_ref[...]
    halo = jnp.where(i > 0, halo_ref[...], 0.0)
    ext = jnp.concatenate([halo, z], axis=0)
    t = (i * tp + lax.broadcasted_iota(jnp.int32, (tp, 1), 0) + 1).astype(F32)

    for gi, wnd in enumerate(POOL_WINDOWS):
        @pl.when(g == gi)
        def _(wnd=wnd):
            win = ext
            d = 1
            while d < wnd:
                win = win + pltpu.roll(win, d, axis=0)
                d *= 2
            pooled = win[POOL_HALO:] / jnp.minimum(t, float(wnd)) - z
            y = jnp.dot(pooled.astype(BF16), w_ref[0], preferred_element_type=F32) * sc_ref[...]
            o_ref[...] = y.astype(o_ref.dtype)


def multiscale_pool(zp, pool_w, pool_scale, *, tp=1024):
    s, width = zp.shape
    ng, gd, _ = pool_w.shape
    body = functools.partial(_pool_body, tp=tp)
    halo_blocks = tp // POOL_HALO
    vmem = 2 * tp * gd * 4 + 2 * gd * gd * 2 + 2 * tp * gd * 2 + 10 * (tp + POOL_HALO) * gd * 4
    return pl.pallas_call(
        body,
        grid=(ng, s // tp),
        in_specs=[pl.BlockSpec((tp, gd), lambda g, i: (i, g)),
                  pl.BlockSpec((POOL_HALO, gd), lambda g, i: (jnp.maximum(i * halo_blocks - 1, 0), g)),
                  pl.BlockSpec((1, gd, gd), lambda g, i: (g, 0, 0)),
                  pl.BlockSpec((1, gd), lambda g, i: (0, g))],
        out_specs=pl.BlockSpec((tp, gd), lambda g, i: (i, g)),
        out_shape=jax.ShapeDtypeStruct((s, width), BF16),
        compiler_params=_params(("arbitrary", "arbitrary"), vmem),
        name="multiscale_pool",
    )(zp, zp, pool_w, pool_scale.reshape(1, width))


def _out_body(a1_ref, a2_ref, w1_ref, w2_ref, r_ref, o_ref):
    d = jnp.dot(a1_ref[...], w1_ref[...], preferred_element_type=F32)
    d = d + jnp.dot(a2_ref[...], w2_ref[...], preferred_element_type=F32)
    o_ref[...] = r_ref[...] + d


def out_proj_residual(a1, a2, w, res, *, tm=1024, tn=512):
    s, k1 = a1.shape
    k2 = a2.shape[1]
    assert k1 == k2
    n = w.shape[1]
    vmem = 2 * tm * k1 * 2 + 4 * k1 * tn * 2 + 4 * tm * tn * 4 + 3 * tm * tn * 4
    return pl.pallas_call(
        _out_body,
        grid=(s // tm, n // tn),
        in_specs=[_single((tm, k1), lambda i, j: (i, 0)),
                  _single((tm, k2), lambda i, j: (i, 0)),
                  pl.BlockSpec((k1, tn), lambda i, j: (0, j)),
                  pl.BlockSpec((k2, tn), lambda i, j: (1, j)),
                  pl.BlockSpec((tm, tn), lambda i, j: (i, j))],
        out_specs=pl.BlockSpec((tm, tn), lambda i, j: (i, j)),
        out_shape=jax.ShapeDtypeStruct((s, n), F32),
        compiler_params=_params(("arbitrary", "arbitrary"), vmem),
        name="out_proj",
    )(a1, a2, w, w, res)


def _causal_conv3(x, prev, w_ref):
    ext = jnp.concatenate([prev, x], axis=0)
    x1 = pltpu.roll(ext, 1, axis=0)[SUBLANES:]
    x2 = pltpu.roll(ext, 2, axis=0)[SUBLANES:]
    return w_ref[0:1, :] * x2 + w_ref[1:2, :] * x1 + w_ref[2:3, :] * x


def _zero_halo_at_first_row_block(halo_ref, j):
    @pl.when(pl.program_id(0) == 0)
    def _():
        halo_ref[j] = jnp.zeros(halo_ref.shape[1:], halo_ref.dtype)


def _ffn_up_body(a_ref, wg_ref, wu_ref, cw_ref, o_ref, halo_ref):
    j = pl.program_id(1)
    _zero_halo_at_first_row_block(halo_ref, j)
    a = a_ref[...]
    g = jnp.dot(a, wg_ref[...], preferred_element_type=F32)
    u = jnp.dot(a, wu_ref[...], preferred_element_type=F32)
    c = _causal_conv3(g, halo_ref[j], cw_ref)
    halo_ref[j] = g[-SUBLANES:, :]
    o_ref[...] = ((c * jax.nn.sigmoid(c)) * u).astype(o_ref.dtype)


def ffn_up(h, w_up, conv_w, *, tm=2048, tn=256):
    s, kdim = h.shape
    d_ff = w_up.shape[1] // 2
    nj = d_ff // tn
    vmem = tm * kdim * 2 + 4 * kdim * tn * 2 + 2 * tm * tn * 2 + 10 * tm * tn * 4
    return pl.pallas_call(
        _ffn_up_body,
        grid=(s // tm, nj),
        in_specs=[_single((tm, kdim), lambda i, j: (i, 0)),
                  pl.BlockSpec((kdim, tn), lambda i, j: (0, j)),
                  pl.BlockSpec((kdim, tn), lambda i, j: (0, nj + j)),
                  pl.BlockSpec((3, tn), lambda i, j: (0, j))],
        out_specs=pl.BlockSpec((tm, tn), lambda i, j: (i, j)),
        out_shape=jax.ShapeDtypeStruct((s, d_ff), BF16),
        scratch_shapes=[pltpu.VMEM((nj, SUBLANES, tn), F32)],
        compiler_params=_params(("arbitrary", "arbitrary"), vmem),
        name="ffn_up",
    )(h, w_up, w_up, conv_w)


def _down_body(a_ref, w_ref, r_ref, o_ref):
    o_ref[...] = r_ref[...] + jnp.dot(a_ref[...], w_ref[...], preferred_element_type=F32)


def ffn_down_residual(act, w_down, res, *, tm=1024, tn=256):
    s, kdim = act.shape
    n = w_down.shape[1]
    vmem = tm * kdim * 2 + 2 * kdim * tn * 2 + 4 * tm * tn * 4 + 2 * tm * tn * 4
    return pl.pallas_call(
        _down_body,
        grid=(s // tm, n // tn),
        in_specs=[_single((tm, kdim), lambda i, j: (i, 0)),
                  pl.BlockSpec((kdim, tn), lambda i, j: (0, j)),
                  pl.BlockSpec((tm, tn), lambda i, j: (i, j))],
        out_specs=pl.BlockSpec((tm, tn), lambda i, j: (i, j)),
        out_shape=jax.ShapeDtypeStruct((s, n), F32),
        compiler_params=_params(("arbitrary", "arbitrary"), vmem),
        name="ffn_down",
    )(act, w_down, res)


def _gconv_body(a_ref, wv_ref, wb_ref, wc_ref, cw_ref, o_ref, halo_ref):
    j = pl.program_id(1)
    _zero_halo_at_first_row_block(halo_ref, j)
    a = a_ref[...]
    v = jnp.dot(a, wv_ref[...], preferred_element_type=F32)
    gate_b = jnp.dot(a, wb_ref[...], preferred_element_type=F32)
    gate_c = jnp.dot(a, wc_ref[...], preferred_element_type=F32)
    x = gate_c * v
    c = _causal_conv3(x, halo_ref[j], cw_ref)
    halo_ref[j] = x[-SUBLANES:, :]
    o_ref[...] = (gate_b * c).astype(o_ref.dtype)


def gated_conv_proj(h, w_in, conv_w, *, tm=1024, tn=256):
    s, kdim = h.shape
    width = conv_w.shape[1]
    nj = width // tn
    vmem = tm * kdim * 2 + 6 * kdim * tn * 2 + 2 * tm * tn * 2 + 12 * tm * tn * 4
    return pl.pallas_call(
        _gconv_body,
        grid=(s // tm, nj),
        in_specs=[_single((tm, kdim), lambda i, j: (i, 0)),
                  pl.BlockSpec((kdim, tn), lambda i, j: (0, j)),
                  pl.BlockSpec((kdim, tn), lambda i, j: (0, nj + j)),
                  pl.BlockSpec((kdim, tn), lambda i, j: (0, 2 * nj + j)),
                  pl.BlockSpec((3, tn), lambda i, j: (0, j))],
        out_specs=pl.BlockSpec((tm, tn), lambda i, j: (i, j)),
        out_shape=jax.ShapeDtypeStruct((s, width), BF16),
        scratch_shapes=[pltpu.VMEM((nj, SUBLANES, tn), F32)],
        compiler_params=_params(("arbitrary", "arbitrary"), vmem),
        name="gated_conv",
    )(h, w_in, w_in, w_in, conv_w)


SQRT_HALF = math.sqrt(0.5)


def _gelu_mm_body(a_ref, w_ref, o_ref):
    z = jnp.dot(a_ref[...], w_ref[...], preferred_element_type=F32)
    o_ref[...] = 0.5 * z * (1.0 + lax.erf(z * SQRT_HALF))


def gelu_proj(h, w_in, *, col_block_offset, n_out, tm=1024, tn=512):
    s, kdim = h.shape
    vmem = tm * kdim * 2 + 2 * kdim * tn * 2 + 2 * tm * tn * 4 + 6 * tm * tn * 4
    return pl.pallas_call(
        _gelu_mm_body,
        grid=(s // tm, n_out // tn),
        in_specs=[_single((tm, kdim), lambda i, j: (i, 0)),
                  pl.BlockSpec((kdim, tn), lambda i, j: (0, j + col_block_offset))],
        out_specs=pl.BlockSpec((tm, tn), lambda i, j: (i, j)),
        out_shape=jax.ShapeDtypeStruct((s, n_out), F32),
        compiler_params=_params(("arbitrary", "arbitrary"), vmem),
        name="gelu_proj",
    )(h, w_in)


def _sgu_body(u_ref, v_ref, g_ref, w_ref, bt_ref, o_ref, *, ts, n_heads):
    v = v_ref[...]
    ms = jnp.mean(v * v, axis=-1, keepdims=True)
    vn = ((v * lax.rsqrt(ms + EPS)) * g_ref[...]).astype(BF16)
    r = lax.broadcasted_iota(jnp.int32, (SGU_CHUNK, SGU_CHUNK), 0)
    c = lax.broadcasted_iota(jnp.int32, (SGU_CHUNK, SGU_CHUNK), 1)
    for h in range(n_heads):
        cols = slice(h * HEAD_DIM, (h + 1) * HEAD_DIM)
        w_tril = jnp.where(r >= c, w_ref[h], 0.0).astype(BF16)
        bias = bt_ref[:, h:h + 1]
        for n in range(ts // SGU_CHUNK):
            rows = slice(n * SGU_CHUNK, (n + 1) * SGU_CHUNK)
            mixed = jnp.dot(w_tril, vn[rows, cols], preferred_element_type=F32) + bias
            o_ref[rows, cols] = (u_ref[rows, cols] * mixed).astype(o_ref.dtype)


def spatial_gating(z, norm_g, w_s, b_s, *, ts=256):
    s, two_w = z.shape
    width = two_w // 2
    n_heads = w_s.shape[0]
    body = functools.partial(_sgu_body, ts=ts, n_heads=n_heads)
    vmem = 4 * ts * width * 4 + 2 * ts * width * 2 + 2 * n_heads * SGU_CHUNK * SGU_CHUNK * 4 + 6 * ts * width * 4
    return pl.pallas_call(
        body,
        grid=(s // ts,),
        in_specs=[pl.BlockSpec((ts, width), lambda i: (i, 0)),
                  pl.BlockSpec((ts, width), lambda i: (i, 1)),
                  pl.BlockSpec((1, width), lambda i: (0, 0)),
                  pl.BlockSpec((n_heads, SGU_CHUNK, SGU_CHUNK), lambda i: (0, 0, 0)),
                  pl.BlockSpec((SGU_CHUNK, n_heads), lambda i: (0, 0))],
        out_specs=pl.BlockSpec((ts, width), lambda i: (i, 0)),
        out_shape=jax.ShapeDtypeStruct((s, width), BF16),
        compiler_params=_params(("arbitrary",), vmem),
        name="spatial_gating",
    )(z, z, norm_g.reshape(1, width), w_s, b_s.T)


def _conv_glu_ffn(x, f_norm, f_up, f_conv, f_down):
    hn = rmsnorm_rows(x, f_norm)
    act = ffn_up(hn, f_up.astype(BF16), f_conv)
    return ffn_down_residual(act, f_down.astype(BF16), x)


def kernel(x, l0_mix_norm, l0_w_in, l0_b_f, l0_q_gain, l0_k_gain, l0_pool_w, l0_pool_scale, l0_w_out, l0_ffn_norm, l0_ffn_w_up, l0_ffn_conv, l0_ffn_w_down, l1_mix_norm, l1_w_in, l1_conv_w, l1_sgu_norm, l1_sgu_w, l1_sgu_b, l1_w_out, l1_ffn_norm, l1_ffn_w_up, l1_ffn_conv, l1_ffn_w_down):
    b, s, d = x.shape
    assert b == 1
    x0 = x.reshape(s, d)
    half = d // 2
    n_heads = half // HEAD_DIM
    lanes = 128

    w_in0 = l0_w_in.astype(BF16)
    w_qkv = w_in0[:, :3 * half]
    w_f = jnp.pad(w_in0[:, 3 * half:3 * half + n_heads], ((0, 0), (0, lanes - n_heads)))
    w_zp = w_in0[:, 3 * half + n_heads:]
    b_f = jnp.pad(l0_b_f, (0, lanes - n_heads)).reshape(1, lanes)
    qk_gain = jnp.concatenate([jnp.tile(l0_q_gain, n_heads), jnp.tile(l0_k_gain, n_heads)]).reshape(1, 2 * half)

    h0 = rmsnorm_rows(x0, l0_mix_norm)
    qkv = qkv_proj(h0, w_qkv, qk_gain)
    cum, cum_t = forget_gate(h0, w_f, b_f, n_heads)
    attn = fox_attention(qkv, cum, cum_t, n_heads)
    zp = matmul(h0, w_zp, out_dtype=F32)
    pm = multiscale_pool(zp, l0_pool_w.astype(BF16), l0_pool_scale)
    x1 = out_proj_residual(attn, pm, l0_w_out.astype(BF16), x0)
    x2 = _conv_glu_ffn(x1, l0_ffn_norm, l0_ffn_w_up, l0_ffn_conv, l0_ffn_w_down)

    w_in1 = l1_w_in.astype(BF16)
    h1 = rmsnorm_rows(x2, l1_mix_norm)
    c_out = gated_conv_proj(h1, w_in1, l1_conv_w)
    z = gelu_proj(h1, w_in1, col_block_offset=(3 * half) // 512, n_out=2 * half)
    d_out = spatial_gating(z, l1_sgu_norm, l1_sgu_w, l1_sgu_b)
    x3 = out_proj_residual(c_out, d_out, l1_w_out.astype(BF16), x2)
    x4 = _conv_glu_ffn(x3, l1_ffn_norm, l1_ffn_w_up, l1_ffn_conv, l1_ffn_w_down)
    return x4.reshape(b, s, d)
```

```python
import functools
import math

import jax
import jax.numpy as jnp
from jax import lax
from jax.experimental import pallas as pl
from jax.experimental.pallas import tpu as pltpu

F32 = jnp.float32
BF16 = jnp.bfloat16

EPS = 1e-6
HEAD_DIM = 128
POOL_WINDOWS = (2, 4, 8, 16)
SGU_CHUNK = 128
NEG_BIG = -1e30

V7X_VMEM_BYTES = 64 << 20
VMEM_CAP = V7X_VMEM_BYTES - (6 << 20)
SUBLANES = 8


def _params(sem, vmem_bytes):
    return pltpu.CompilerParams(
        dimension_semantics=sem,
        vmem_limit_bytes=int(min(max(vmem_bytes, 16 << 20), VMEM_CAP)),
    )


def _single(block, index_map):
    return pl.BlockSpec(block, index_map, pipeline_mode=pl.Buffered(1))


def _rmsnorm_body(x_ref, g_ref, o_ref):
    x = x_ref[...]
    ms = jnp.mean(x * x, axis=-1, keepdims=True)
    o_ref[...] = ((x * lax.rsqrt(ms + EPS)) * g_ref[...]).astype(o_ref.dtype)


def rmsnorm_rows(x, g, *, tr=256):
    s, d = x.shape
    return pl.pallas_call(
        _rmsnorm_body,
        grid=(s // tr,),
        in_specs=[pl.BlockSpec((tr, d), lambda i: (i, 0)),
                  pl.BlockSpec((1, d), lambda i: (0, 0))],
        out_specs=pl.BlockSpec((tr, d), lambda i: (i, 0)),
        out_shape=jax.ShapeDtypeStruct((s, d), BF16),
        compiler_params=_params(("arbitrary",), 2 * tr * d * 4 + 2 * tr * d * 2 + 4 * tr * d * 4),
        name="rmsnorm",
    )(x, g.reshape(1, d))


def _qkv_body(a_ref, w_ref, g_ref, o_ref, *, n_norm_tiles, heads_per_tile):
    j = pl.program_id(1)
    acc = jnp.dot(a_ref[...], w_ref[...].astype(BF16), preferred_element_type=F32)

    @pl.when(j < n_norm_tiles)
    def _():
        for hh in range(heads_per_tile):
            sl = slice(hh * HEAD_DIM, (hh + 1) * HEAD_DIM)
            blk = acc[:, sl]
            ms = jnp.mean(blk * blk, axis=-1, keepdims=True)
            o_ref[:, sl] = ((blk * lax.rsqrt(ms + EPS)) * g_ref[:, sl]).astype(o_ref.dtype)

    @pl.when(j >= n_norm_tiles)
    def _():
        o_ref[...] = acc.astype(o_ref.dtype)


def qkv_proj(h, w_in, qk_gain, *, n, tm=1024, tn=512):
    s, kdim = h.shape
    n_qk = qk_gain.shape[1]
    n_norm_tiles = n_qk // tn
    body = functools.partial(_qkv_body, n_norm_tiles=n_norm_tiles, heads_per_tile=tn // HEAD_DIM)
    vmem = tm * kdim * 2 + 2 * kdim * tn * 4 + kdim * tn * 2 + 2 * tm * tn * 2 + 4 * tm * tn * 4
    return pl.pallas_call(
        body,
        grid=(s // tm, n // tn),
        in_specs=[_single((tm, kdim), lambda i, j: (i, 0)),
                  pl.BlockSpec((kdim, tn), lambda i, j: (0, j)),
                  pl.BlockSpec((1, tn), lambda i, j: (0, jnp.minimum(j, n_norm_tiles - 1)))],
        out_specs=pl.BlockSpec((tm, tn), lambda i, j: (i, j)),
        out_shape=jax.ShapeDtypeStruct((s, n), BF16),
        compiler_params=_params(("arbitrary", "arbitrary"), vmem),
        name="qkv_proj",
    )(h, w_in, qk_gain)


def _mm_body(a_ref, w_ref, o_ref):
    o_ref[...] = jnp.dot(a_ref[...], w_ref[...], preferred_element_type=F32).astype(o_ref.dtype)


def matmul(a, w, *, out_dtype, tm=1024, tn=512, col_block_offset=0, n_out=None):
    s, kdim = a.shape
    n = w.shape[1] if n_out is None else n_out
    osz = jnp.dtype(out_dtype).itemsize
    vmem = tm * kdim * 2 + 2 * kdim * tn * 2 + 2 * tm * tn * osz + 2 * tm * tn * 4
    return pl.pallas_call(
        _mm_body,
        grid=(s // tm, n // tn),
        in_specs=[_single((tm, kdim), lambda i, j: (i, 0)),
                  pl.BlockSpec((kdim, tn), lambda i, j: (0, j + col_block_offset))],
        out_specs=pl.BlockSpec((tm, tn), lambda i, j: (i, j)),
        out_shape=jax.ShapeDtypeStruct((s, n), out_dtype),
        compiler_params=_params(("arbitrary", "arbitrary"), vmem),
        name="matmul",
    )(a, w)


def _cumsum_rows(x):
    n = x.shape[0]
    row = lax.broadcasted_iota(jnp.int32, x.shape, 0)
    d = 1
    while d < n:
        x = x + jnp.where(row >= d, pltpu.roll(x, d, axis=0), 0.0)
        d *= 2
    return x


def _gate_body(a_ref, w_ref, b_ref, cum_ref, cumt_ref, carry_ref, *, n_heads):
    i = pl.program_id(0)

    @pl.when(i == 0)
    def _():
        carry_ref[...] = jnp.zeros_like(carry_ref)

    f = jnp.dot(a_ref[...], w_ref[...], preferred_element_type=F32) + b_ref[...]
    log_f = jnp.minimum(f, 0.0) - jnp.log1p(jnp.exp(-jnp.abs(f)))
    cum = _cumsum_rows(log_f) + carry_ref[...]
    cum_ref[...] = cum
    cumt_ref[...] = jnp.transpose(cum)[:n_heads, :]
    carry_ref[...] = cum[-1:, :]


def forget_gate(h, w_f, b_f, n_heads, *, tg=1024):
    s, kdim = h.shape
    lanes = w_f.shape[1]
    body = functools.partial(_gate_body, n_heads=n_heads)
    vmem = 2 * tg * kdim * 2 + 2 * kdim * lanes * 2 + 16 * tg * lanes * 4
    return pl.pallas_call(
        body,
        grid=(s // tg,),
        in_specs=[pl.BlockSpec((tg, kdim), lambda i: (i, 0)),
                  pl.BlockSpec((kdim, lanes), lambda i: (0, 0)),
                  pl.BlockSpec((1, lanes), lambda i: (0, 0))],
        out_specs=[pl.BlockSpec((tg, lanes), lambda i: (i, 0)),
                   pl.BlockSpec((n_heads, tg), lambda i: (0, i))],
        out_shape=[jax.ShapeDtypeStruct((s, lanes), F32),
                   jax.ShapeDtypeStruct((n_heads, s), F32)],
        scratch_shapes=[pltpu.VMEM((1, lanes), F32)],
        compiler_params=_params(("arbitrary",), vmem),
        name="forget_gate",
    )(h, w_f, b_f)


LOG2E = math.log2(math.e)


def _attn_body(q_ref, k_ref, v_ref, cq_ref, ck_ref, o_ref, m_ref, l_ref, acc_ref, *, tq, tk, gh, c1):
    hg = pl.program_id(0)
    qb = pl.program_id(1)
    lane = lax.broadcasted_iota(jnp.int32, (tq, HEAD_DIM), 1)
    cq_blk = cq_ref[...] * LOG2E
    cq2 = [jnp.sum(jnp.where(lane == hg * gh + g, cq_blk, 0.0), axis=1, keepdims=True)
           for g in range(gh)]

    m_ref[...] = jnp.full_like(m_ref, NEG_BIG)
    l_ref[...] = jnp.zeros_like(l_ref)
    acc_ref[...] = jnp.zeros_like(acc_ref)

    def block(kb, masked):
        start = pl.multiple_of(kb * tk, tk)
        if masked:
            row = lax.broadcasted_iota(jnp.int32, (tq, tk), 0)
            col = lax.broadcasted_iota(jnp.int32, (tq, tk), 1)
            keep = col <= row
        for g in range(gh):
            cols = slice(g * HEAD_DIM, (g + 1) * HEAD_DIM)
            k = k_ref[pl.ds(start, tk), cols]
            v = v_ref[pl.ds(start, tk), cols]
            s = lax.dot_general(q_ref[:, cols], k, (((1,), (1,)), ((), ())),
                                preferred_element_type=F32)
            u = s * c1 - ck_ref[g, kb] * LOG2E
            if masked:
                u = jnp.where(keep, u, NEG_BIG)
            m_old = m_ref[g]
            m_new = jnp.maximum(m_old, jnp.max(u, axis=1, keepdims=True) + cq2[g])
            alpha = jnp.exp2(m_old - m_new)
            d = m_new - cq2[g]
            p = [jnp.exp2(u[:, c * HEAD_DIM:(c + 1) * HEAD_DIM] - d) for c in range(tk // HEAD_DIM)]
            l_ref[g] = alpha * l_ref[g] + functools.reduce(lambda a, b: a + b, p)
            pb = jnp.concatenate([x.astype(BF16) for x in p], axis=1)
            acc_ref[g] = alpha * acc_ref[g] + jnp.dot(pb, v, preferred_element_type=F32)
            m_ref[g] = m_new

    def full_block(kb, carry):
        block(kb, False)
        return carry

    lax.fori_loop(0, qb, full_block, 0)
    block(qb, True)
    for g in range(gh):
        l = jnp.sum(l_ref[g], axis=1, keepdims=True)
        o_ref[:, g * HEAD_DIM:(g + 1) * HEAD_DIM] = (acc_ref[g] / l).astype(o_ref.dtype)


def fox_attention(qkv, cum, cum_t, n_heads, *, tq=512, gh=4):
    s = qkv.shape[0]
    tk = tq
    nk = s // tk
    ng = n_heads // gh
    gw = gh * HEAD_DIM
    ck = cum_t.reshape(n_heads, nk, 1, tk)
    body = functools.partial(_attn_body, tq=tq, tk=tk, gh=gh,
                             c1=LOG2E / math.sqrt(HEAD_DIM))
    vmem = (2 * s * gw * 2 + 2 * tq * gw * 2 + 2 * tq * HEAD_DIM * 4 + 2 * gh * s * 4
            + 2 * tq * gw * 2 + 3 * gh * tq * HEAD_DIM * 4 + 6 * gh * tq * tk * 4)
    return pl.pallas_call(
        body,
        grid=(ng, s // tq),
        in_specs=[pl.BlockSpec((tq, gw), lambda h, i: (i, h)),
                  _single((s, gw), lambda h, i: (0, ng + h)),
                  _single((s, gw), lambda h, i: (0, 2 * ng + h)),
                  pl.BlockSpec((tq, HEAD_DIM), lambda h, i: (i, 0)),
                  pl.BlockSpec((gh, nk, 1, tk), lambda h, i: (h, 0, 0, 0))],
        out_specs=pl.BlockSpec((tq, gw), lambda h, i: (i, h)),
        out_shape=jax.ShapeDtypeStruct((s, n_heads * HEAD_DIM), BF16),
        scratch_shapes=[pltpu.VMEM((gh, tq, HEAD_DIM), F32), pltpu.VMEM((gh, tq, HEAD_DIM), F32),
                        pltpu.VMEM((gh, tq, HEAD_DIM), F32)],
        compiler_params=_params(("arbitrary", "arbitrary"), vmem),
        name="fox_attention",
    )(qkv, qkv, qkv, cum, ck)


POOL_HALO = 16


def _pool_body(z_ref, halo_ref, w_ref, sc_ref, o_ref, *, tp):
    g = pl.program_id(0)
    i = pl.program_id(1)
    z = z_ref[...]
    halo = jnp.where(i > 0, halo_ref[...], 0.0)
    ext = jnp.concatenate([halo, z], axis=0)
    t = (i * tp + lax.broadcasted_iota(jnp.int32, (tp, 1), 0) + 1).astype(F32)

    for gi, wnd in enumerate(POOL_WINDOWS):
        @pl.when(g == gi)
        def _(wnd=wnd):
            win = ext
            d = 1
            while d < wnd:
                win = win + pltpu.roll(win, d, axis=0)
                d *= 2
            pooled = win[POOL_HALO:] / jnp.minimum(t, float(wnd)) - z
            y = jnp.dot(pooled.astype(BF16), w_ref[0], preferred_element_type=F32) * sc_ref[...]
            o_ref[...] = y.astype(o_ref.dtype)


def multiscale_pool(zp, pool_w, pool_scale, *, tp=1024):
    s, width = zp.shape
    ng, gd, _ = pool_w.shape
    body = functools.partial(_pool_body, tp=tp)
    halo_blocks = tp // POOL_HALO
    vmem = 2 * tp * gd * 4 + 2 * gd * gd * 2 + 2 * tp * gd * 2 + 10 * (tp + POOL_HALO) * gd * 4
    return pl.pallas_call(
        body,
        grid=(ng, s // tp),
        in_specs=[pl.BlockSpec((tp, gd), lambda g, i: (i, g)),
                  pl.BlockSpec((POOL_HALO, gd), lambda g, i: (jnp.maximum(i * halo_blocks - 1, 0), g)),
                  pl.BlockSpec((1, gd, gd), lambda g, i: (g, 0, 0)),
                  pl.BlockSpec((1, gd), lambda g, i: (0, g))],
        out_specs=pl.BlockSpec((tp, gd), lambda g, i: (i, g)),
        out_shape=jax.ShapeDtypeStruct((s, width), BF16),
        compiler_params=_params(("arbitrary", "arbitrary"), vmem),
        name="multiscale_pool",
    )(zp, zp, pool_w, pool_scale.reshape(1, width))


def _out_body(a1_ref, a2_ref, w1_ref, w2_ref, r_ref, o_ref):
    d = jnp.dot(a1_ref[...], w1_ref[...], preferred_element_type=F32)
    d = d + jnp.dot(a2_ref[...], w2_ref[...], preferred_element_type=F32)
    o_ref[...] = r_ref[...] + d


def out_proj_residual(a1, a2, w, res, *, tm=1024, tn=1024):
    s, k1 = a1.shape
    k2 = a2.shape[1]
    assert k1 == k2
    n = w.shape[1]
    vmem = 2 * tm * k1 * 2 + 4 * k1 * tn * 2 + 4 * tm * tn * 4 + 3 * tm * tn * 4
    return pl.pallas_call(
        _out_body,
        grid=(s // tm, n // tn),
        in_specs=[_single((tm, k1), lambda i, j: (i, 0)),
                  _single((tm, k2), lambda i, j: (i, 0)),
                  pl.BlockSpec((k1, tn), lambda i, j: (0, j)),
                  pl.BlockSpec((k2, tn), lambda i, j: (1, j)),
                  pl.BlockSpec((tm, tn), lambda i, j: (i, j))],
        out_specs=pl.BlockSpec((tm, tn), lambda i, j: (i, j)),
        out_shape=jax.ShapeDtypeStruct((s, n), F32),
        compiler_params=_params(("arbitrary", "arbitrary"), vmem),
        name="out_proj",
    )(a1, a2, w, w, res)


def _causal_conv3(x, prev, w_ref):
    ext = jnp.concatenate([prev, x], axis=0)
    x1 = pltpu.roll(ext, 1, axis=0)[SUBLANES:]
    x2 = pltpu.roll(ext, 2, axis=0)[SUBLANES:]
    return w_ref[0:1, :] * x2 + w_ref[1:2, :] * x1 + w_ref[2:3, :] * x


def _zero_halo_at_first_row_block(halo_ref, j):
    @pl.when(pl.program_id(0) == 0)
    def _():
        halo_ref[j] = jnp.zeros(halo_ref.shape[1:], halo_ref.dtype)


def _ffn_up_body(a_ref, wg_ref, wu_ref, cw_ref, o_ref, halo_ref, *, row_chunks):
    j = pl.program_id(1)
    _zero_halo_at_first_row_block(halo_ref, j)
    wg = wg_ref[...].astype(BF16)
    wu = wu_ref[...].astype(BF16)
    tc = a_ref.shape[0] // row_chunks
    prev = halo_ref[j]
    for rc in range(row_chunks):
        rows = slice(rc * tc, (rc + 1) * tc)
        a = a_ref[rows, :]
        g = jnp.dot(a, wg, preferred_element_type=F32)
        u = jnp.dot(a, wu, preferred_element_type=F32)
        c = _causal_conv3(g, prev, cw_ref)
        prev = g[-SUBLANES:, :]
        o_ref[rows, :] = ((c * jax.nn.sigmoid(c)) * u).astype(o_ref.dtype)
    halo_ref[j] = prev


def ffn_up(h, w_up, conv_w, *, tm=2048, tn=256, row_chunks=2):
    s, kdim = h.shape
    d_ff = w_up.shape[1] // 2
    nj = d_ff // tn
    tc = tm // row_chunks
    vmem = (tm * kdim * 2 + 4 * kdim * tn * 4 + 2 * kdim * tn * 2 + 2 * tm * tn * 2
            + 10 * tc * tn * 4)
    return pl.pallas_call(
        functools.partial(_ffn_up_body, row_chunks=row_chunks),
        grid=(s // tm, nj),
        in_specs=[_single((tm, kdim), lambda i, j: (i, 0)),
                  pl.BlockSpec((kdim, tn), lambda i, j: (0, j)),
                  pl.BlockSpec((kdim, tn), lambda i, j: (0, nj + j)),
                  pl.BlockSpec((3, tn), lambda i, j: (0, j))],
        out_specs=pl.BlockSpec((tm, tn), lambda i, j: (i, j)),
        out_shape=jax.ShapeDtypeStruct((s, d_ff), BF16),
        scratch_shapes=[pltpu.VMEM((nj, SUBLANES, tn), F32)],
        compiler_params=_params(("arbitrary", "arbitrary"), vmem),
        name="ffn_up",
    )(h, w_up, w_up, conv_w)


def _down_body(a_ref, w_ref, r_ref, o_ref):
    o_ref[...] = r_ref[...] + jnp.dot(a_ref[...], w_ref[...], preferred_element_type=F32)


def ffn_down_residual(act, w_down, res, *, tm=1024, tn=256):
    s, kdim = act.shape
    n = w_down.shape[1]
    vmem = tm * kdim * 2 + 2 * kdim * tn * 2 + 4 * tm * tn * 4 + 2 * tm * tn * 4
    return pl.pallas_call(
        _down_body,
        grid=(s // tm, n // tn),
        in_specs=[_single((tm, kdim), lambda i, j: (i, 0)),
                  pl.BlockSpec((kdim, tn), lambda i, j: (0, j)),
                  pl.BlockSpec((tm, tn), lambda i, j: (i, j))],
        out_specs=pl.BlockSpec((tm, tn), lambda i, j: (i, j)),
        out_shape=jax.ShapeDtypeStruct((s, n), F32),
        compiler_params=_params(("arbitrary", "arbitrary"), vmem),
        name="ffn_down",
    )(act, w_down, res)


def _gconv_body(a_ref, wv_ref, wb_ref, wc_ref, cw_ref, o_ref, halo_ref, *, row_chunks):
    j = pl.program_id(1)
    _zero_halo_at_first_row_block(halo_ref, j)
    wv = wv_ref[...].astype(BF16)
    wb = wb_ref[...].astype(BF16)
    wc = wc_ref[...].astype(BF16)
    tc = a_ref.shape[0] // row_chunks
    prev = halo_ref[j]
    for rc in range(row_chunks):
        rows = slice(rc * tc, (rc + 1) * tc)
        a = a_ref[rows, :]
        v = jnp.dot(a, wv, preferred_element_type=F32)
        gate_b = jnp.dot(a, wb, preferred_element_type=F32)
        gate_c = jnp.dot(a, wc, preferred_element_type=F32)
        x = gate_c * v
        c = _causal_conv3(x, prev, cw_ref)
        prev = x[-SUBLANES:, :]
        o_ref[rows, :] = (gate_b * c).astype(o_ref.dtype)
    halo_ref[j] = prev


def gated_conv_proj(h, w_in, conv_w, *, tm=1024, tn=256, row_chunks=2):
    s, kdim = h.shape
    width = conv_w.shape[1]
    nj = width // tn
    tc = tm // row_chunks
    vmem = (tm * kdim * 2 + 6 * kdim * tn * 4 + 3 * kdim * tn * 2 + 2 * tm * tn * 2
            + 12 * tc * tn * 4)
    return pl.pallas_call(
        functools.partial(_gconv_body, row_chunks=row_chunks),
        grid=(s // tm, nj),
        in_specs=[_single((tm, kdim), lambda i, j: (i, 0)),
                  pl.BlockSpec((kdim, tn), lambda i, j: (0, j)),
                  pl.BlockSpec((kdim, tn), lambda i, j: (0, nj + j)),
                  pl.BlockSpec((kdim, tn), lambda i, j: (0, 2 * nj + j)),
                  pl.BlockSpec((3, tn), lambda i, j: (0, j))],
        out_specs=pl.BlockSpec((tm, tn), lambda i, j: (i, j)),
        out_shape=jax.ShapeDtypeStruct((s, width), BF16),
        scratch_shapes=[pltpu.VMEM((nj, SUBLANES, tn), F32)],
        compiler_params=_params(("arbitrary", "arbitrary"), vmem),
        name="gated_conv",
    )(h, w_in, w_in, w_in, conv_w)


SQRT_HALF = math.sqrt(0.5)


def _gelu_mm_body(a_ref, w_ref, o_ref):
    z = jnp.dot(a_ref[...], w_ref[...].astype(BF16), preferred_element_type=F32)
    o_ref[...] = 0.5 * z * (1.0 + lax.erf(z * SQRT_HALF))


def gelu_proj(h, w_in, *, col_block_offset, n_out, tm=1024, tn=512):
    s, kdim = h.shape
    vmem = tm * kdim * 2 + 2 * kdim * tn * 4 + kdim * tn * 2 + 2 * tm * tn * 4 + 6 * tm * tn * 4
    return pl.pallas_call(
        _gelu_mm_body,
        grid=(s // tm, n_out // tn),
        in_specs=[_single((tm, kdim), lambda i, j: (i, 0)),
                  pl.BlockSpec((kdim, tn), lambda i, j: (0, j + col_block_offset))],
        out_specs=pl.BlockSpec((tm, tn), lambda i, j: (i, j)),
        out_shape=jax.ShapeDtypeStruct((s, n_out), F32),
        compiler_params=_params(("arbitrary", "arbitrary"), vmem),
        name="gelu_proj",
    )(h, w_in)


def _sgu_body(u_ref, v_ref, g_ref, w_ref, bt_ref, o_ref, *, ts, n_heads):
    v = v_ref[...]
    ms = jnp.mean(v * v, axis=-1, keepdims=True)
    vn = ((v * lax.rsqrt(ms + EPS)) * g_ref[...]).astype(BF16)
    r = lax.broadcasted_iota(jnp.int32, (SGU_CHUNK, SGU_CHUNK), 0)
    c = lax.broadcasted_iota(jnp.int32, (SGU_CHUNK, SGU_CHUNK), 1)
    for h in range(n_heads):
        cols = slice(h * HEAD_DIM, (h + 1) * HEAD_DIM)
        w_tril = jnp.where(r >= c, w_ref[h], 0.0).astype(BF16)
        bias = bt_ref[:, h:h + 1]
        for n in range(ts // SGU_CHUNK):
            rows = slice(n * SGU_CHUNK, (n + 1) * SGU_CHUNK)
            mixed = jnp.dot(w_tril, vn[rows, cols], preferred_element_type=F32) + bias
            o_ref[rows, cols] = (u_ref[rows, cols] * mixed).astype(o_ref.dtype)


def spatial_gating(z, norm_g, w_s, b_s, *, ts=256):
    s, two_w = z.shape
    width = two_w // 2
    n_heads = w_s.shape[0]
    body = functools.partial(_sgu_body, ts=ts, n_heads=n_heads)
    vmem = 4 * ts * width * 4 + 2 * ts * width * 2 + 2 * n_heads * SGU_CHUNK * SGU_CHUNK * 4 + 6 * ts * width * 4
    return pl.pallas_call(
        body,
        grid=(s // ts,),
        in_specs=[pl.BlockSpec((ts, width), lambda i: (i, 0)),
                  pl.BlockSpec((ts, width), lambda i: (i, 1)),
                  pl.BlockSpec((1, width), lambda i: (0, 0)),
                  pl.BlockSpec((n_heads, SGU_CHUNK, SGU_CHUNK), lambda i: (0, 0, 0)),
                  pl.BlockSpec((SGU_CHUNK, n_heads), lambda i: (0, 0))],
        out_specs=pl.BlockSpec((ts, width), lambda i: (i, 0)),
        out_shape=jax.ShapeDtypeStruct((s, width), BF16),
        compiler_params=_params(("arbitrary",), vmem),
        name="spatial_gating",
    )(z, z, norm_g.reshape(1, width), w_s, b_s.T)


def _conv_glu_ffn(x, f_norm, f_up, f_conv, f_down):
    hn = rmsnorm_rows(x, f_norm)
    act = ffn_up(hn, f_up, f_conv)
    return ffn_down_residual(act, f_down.astype(BF16), x)


def kernel(x, l0_mix_norm, l0_w_in, l0_b_f, l0_q_gain, l0_k_gain, l0_pool_w, l0_pool_scale, l0_w_out, l0_ffn_norm, l0_ffn_w_up, l0_ffn_conv, l0_ffn_w_down, l1_mix_norm, l1_w_in, l1_conv_w, l1_sgu_norm, l1_sgu_w, l1_sgu_b, l1_w_out, l1_ffn_norm, l1_ffn_w_up, l1_ffn_conv, l1_ffn_w_down):
    b, s, d = x.shape
    assert b == 1
    x0 = x.reshape(s, d)
    half = d // 2
    n_heads = half // HEAD_DIM
    lanes = 128

    w_f = jnp.pad(l0_w_in[:, 3 * half:3 * half + n_heads].astype(BF16), ((0, 0), (0, lanes - n_heads)))
    w_zp = l0_w_in[:, 3 * half + n_heads:].astype(BF16)
    b_f = jnp.pad(l0_b_f, (0, lanes - n_heads)).reshape(1, lanes)
    qk_gain = jnp.concatenate([jnp.tile(l0_q_gain, n_heads), jnp.tile(l0_k_gain, n_heads)]).reshape(1, 2 * half)

    h0 = rmsnorm_rows(x0, l0_mix_norm)
    qkv = qkv_proj(h0, l0_w_in, qk_gain, n=3 * half)
    cum, cum_t = forget_gate(h0, w_f, b_f, n_heads)
    attn = fox_attention(qkv, cum, cum_t, n_heads)
    zp = matmul(h0, w_zp, out_dtype=F32, tn=1024)
    pm = multiscale_pool(zp, l0_pool_w.astype(BF16), l0_pool_scale)
    x1 = out_proj_residual(attn, pm, l0_w_out.astype(BF16), x0)
    x2 = _conv_glu_ffn(x1, l0_ffn_norm, l0_ffn_w_up, l0_ffn_conv, l0_ffn_w_down)

    h1 = rmsnorm_rows(x2, l1_mix_norm)
    c_out = gated_conv_proj(h1, l1_w_in, l1_conv_w)
    z = gelu_proj(h1, l1_w_in, col_block_offset=(3 * half) // 512, n_out=2 * half)
    d_out = spatial_gating(z, l1_sgu_norm, l1_sgu_w, l1_sgu_b)
    x3 = out_proj_residual(c_out, d_out, l1_w_out.astype(BF16), x2)
    x4 = _conv_glu_ffn(x3, l1_ffn_norm, l1_ffn_w_up, l1_ffn_conv, l1_ffn_w_down)
    return x4.reshape(b, s, d)
```

```python
import functools
import math

import jax
import jax.numpy as jnp
from jax import lax
from jax.experimental import pallas as pl
from jax.experimental.pallas import tpu as pltpu

F32 = jnp.float32
BF16 = jnp.bfloat16

EPS = 1e-6
HEAD_DIM = 128
POOL_WINDOWS = (2, 4, 8, 16)
SGU_CHUNK = 128
NEG_BIG = -1e30

V7X_VMEM_BYTES = 64 << 20
VMEM_CAP = V7X_VMEM_BYTES - (6 << 20)
SUBLANES = 8


def _params(sem, vmem_bytes):
    return pltpu.CompilerParams(
        dimension_semantics=sem,
        vmem_limit_bytes=int(min(max(vmem_bytes, 16 << 20), VMEM_CAP)),
    )


def _single(block, index_map):
    return pl.BlockSpec(block, index_map, pipeline_mode=pl.Buffered(1))


def _rmsnorm_body(x_ref, g_ref, o_ref):
    x = x_ref[...]
    ms = jnp.mean(x * x, axis=-1, keepdims=True)
    o_ref[...] = ((x * lax.rsqrt(ms + EPS)) * g_ref[...]).astype(o_ref.dtype)


def rmsnorm_rows(x, g, *, tr=256):
    s, d = x.shape
    return pl.pallas_call(
        _rmsnorm_body,
        grid=(s // tr,),
        in_specs=[pl.BlockSpec((tr, d), lambda i: (i, 0)),
                  pl.BlockSpec((1, d), lambda i: (0, 0))],
        out_specs=pl.BlockSpec((tr, d), lambda i: (i, 0)),
        out_shape=jax.ShapeDtypeStruct((s, d), BF16),
        compiler_params=_params(("arbitrary",), 2 * tr * d * 4 + 2 * tr * d * 2 + 4 * tr * d * 4),
        name="rmsnorm",
    )(x, g.reshape(1, d))


def _qkv_body(a_ref, w_ref, g_ref, o_ref, *, n_norm_tiles, heads_per_tile):
    j = pl.program_id(1)
    acc = jnp.dot(a_ref[...], w_ref[...], preferred_element_type=F32)

    @pl.when(j < n_norm_tiles)
    def _():
        for hh in range(heads_per_tile):
            sl = slice(hh * HEAD_DIM, (hh + 1) * HEAD_DIM)
            blk = acc[:, sl]
            ms = jnp.mean(blk * blk, axis=-1, keepdims=True)
            o_ref[:, sl] = ((blk * lax.rsqrt(ms + EPS)) * g_ref[:, sl]).astype(o_ref.dtype)

    @pl.when(j >= n_norm_tiles)
    def _():
        o_ref[...] = acc.astype(o_ref.dtype)


def qkv_proj(h, w_qkv, qk_gain, *, tm=1024, tn=512):
    s, kdim = h.shape
    n = w_qkv.shape[1]
    n_qk = qk_gain.shape[1]
    n_norm_tiles = n_qk // tn
    body = functools.partial(_qkv_body, n_norm_tiles=n_norm_tiles, heads_per_tile=tn // HEAD_DIM)
    vmem = 2 * tm * kdim * 2 + 2 * kdim * tn * 2 + 2 * tm * tn * 2 + 4 * tm * tn * 4
    return pl.pallas_call(
        body,
        grid=(s // tm, n // tn),
        in_specs=[pl.BlockSpec((tm, kdim), lambda i, j: (i, 0)),
                  pl.BlockSpec((kdim, tn), lambda i, j: (0, j)),
                  pl.BlockSpec((1, tn), lambda i, j: (0, jnp.minimum(j, n_norm_tiles - 1)))],
        out_specs=pl.BlockSpec((tm, tn), lambda i, j: (i, j)),
        out_shape=jax.ShapeDtypeStruct((s, n), BF16),
        compiler_params=_params(("arbitrary", "arbitrary"), vmem),
        name="qkv_proj",
    )(h, w_qkv, qk_gain)


def _cumsum_rows(x):
    n = x.shape[0]
    row = lax.broadcasted_iota(jnp.int32, x.shape, 0)
    d = 1
    while d < n:
        x = x + jnp.where(row >= d, pltpu.roll(x, d, axis=0), 0.0)
        d *= 2
    return x


def _gate_body(a_ref, w_ref, b_ref, cum_ref, cumt_ref, carry_ref, *, n_heads):
    i = pl.program_id(0)

    @pl.when(i == 0)
    def _():
        carry_ref[...] = jnp.zeros_like(carry_ref)

    f = jnp.dot(a_ref[...], w_ref[...], preferred_element_type=F32) + b_ref[...]
    log_f = jnp.minimum(f, 0.0) - jnp.log1p(jnp.exp(-jnp.abs(f)))
    cum = _cumsum_rows(log_f) + carry_ref[...]
    cum_ref[...] = cum
    cumt_ref[...] = jnp.transpose(cum)[:n_heads, :]
    carry_ref[...] = cum[-1:, :]


def forget_gate(h, w_f, b_f, n_heads, *, tg=1024):
    s, kdim = h.shape
    lanes = w_f.shape[1]
    body = functools.partial(_gate_body, n_heads=n_heads)
    vmem = 2 * tg * kdim * 2 + 2 * kdim * lanes * 2 + 16 * tg * lanes * 4
    return pl.pallas_call(
        body,
        grid=(s // tg,),
        in_specs=[pl.BlockSpec((tg, kdim), lambda i: (i, 0)),
                  pl.BlockSpec((kdim, lanes), lambda i: (0, 0)),
                  pl.BlockSpec((1, lanes), lambda i: (0, 0))],
        out_specs=[pl.BlockSpec((tg, lanes), lambda i: (i, 0)),
                   pl.BlockSpec((n_heads, tg), lambda i: (0, i))],
        out_shape=[jax.ShapeDtypeStruct((s, lanes), F32),
                   jax.ShapeDtypeStruct((n_heads, s), F32)],
        scratch_shapes=[pltpu.VMEM((1, lanes), F32)],
        compiler_params=_params(("arbitrary",), vmem),
        name="forget_gate",
    )(h, w_f, b_f)


LOG2E = math.log2(math.e)


def _attn_body(q_ref, k_ref, v_ref, cq_ref, ck_ref, o_ref, m_ref, l_ref, acc_ref, *, tq, tk, gh, c1):
    hg = pl.program_id(0)
    qb = pl.program_id(1)
    lane = lax.broadcasted_iota(jnp.int32, (tq, HEAD_DIM), 1)
    cq_blk = cq_ref[...] * LOG2E
    cq2 = [jnp.sum(jnp.where(lane == hg * gh + g, cq_blk, 0.0), axis=1, keepdims=True)
           for g in range(gh)]

    m_ref[...] = jnp.full_like(m_ref, NEG_BIG)
    l_ref[...] = jnp.zeros_like(l_ref)
    acc_ref[...] = jnp.zeros_like(acc_ref)

    def block(kb, masked):
        start = pl.multiple_of(kb * tk, tk)
        if masked:
            row = lax.broadcasted_iota(jnp.int32, (tq, tk), 0)
            col = lax.broadcasted_iota(jnp.int32, (tq, tk), 1)
            keep = col <= row
        for g in range(gh):
            cols = slice(g * HEAD_DIM, (g + 1) * HEAD_DIM)
            k = k_ref[pl.ds(start, tk), cols]
            v = v_ref[pl.ds(start, tk), cols]
            s = lax.dot_general(q_ref[:, cols], k, (((1,), (1,)), ((), ())),
                                preferred_element_type=F32)
            u = s * c1 - ck_ref[g, kb] * LOG2E
            if masked:
                u = jnp.where(keep, u, NEG_BIG)
            m_old = m_ref[g]
            m_new = jnp.maximum(m_old, jnp.max(u, axis=1, keepdims=True) + cq2[g])
            alpha = jnp.exp2(m_old - m_new)
            d = m_new - cq2[g]
            p = [jnp.exp2(u[:, c * HEAD_DIM:(c + 1) * HEAD_DIM] - d) for c in range(tk // HEAD_DIM)]
            l_ref[g] = alpha * l_ref[g] + functools.reduce(lambda a, b: a + b, p)
            pb = jnp.concatenate([x.astype(BF16) for x in p], axis=1)
            acc_ref[g] = alpha * acc_ref[g] + jnp.dot(pb, v, preferred_element_type=F32)
            m_ref[g] = m_new

    def full_block(kb, carry):
        block(kb, False)
        return carry

    lax.fori_loop(0, qb, full_block, 0)
    block(qb, True)
    for g in range(gh):
        l = jnp.sum(l_ref[g], axis=1, keepdims=True)
        o_ref[:, g * HEAD_DIM:(g + 1) * HEAD_DIM] = (acc_ref[g] / l).astype(o_ref.dtype)


def fox_attention(qkv, cum, cum_t, n_heads, *, tq=512, gh=4):
    s = qkv.shape[0]
    tk = tq
    nk = s // tk
    ng = n_heads // gh
    gw = gh * HEAD_DIM
    ck = cum_t.reshape(n_heads, nk, 1, tk)
    body = functools.partial(_attn_body, tq=tq, tk=tk, gh=gh,
                             c1=LOG2E / math.sqrt(HEAD_DIM))
    vmem = (2 * s * gw * 2 + 2 * tq * gw * 2 + 2 * tq * HEAD_DIM * 4 + 2 * gh * s * 4
            + 2 * tq * gw * 2 + 3 * gh * tq * HEAD_DIM * 4 + 6 * gh * tq * tk * 4)
    return pl.pallas_call(
        body,
        grid=(ng, s // tq),
        in_specs=[pl.BlockSpec((tq, gw), lambda h, i: (i, h)),
                  _single((s, gw), lambda h, i: (0, ng + h)),
                  _single((s, gw), lambda h, i: (0, 2 * ng + h)),
                  pl.BlockSpec((tq, HEAD_DIM), lambda h, i: (i, 0)),
                  pl.BlockSpec((gh, nk, 1, tk), lambda h, i: (h, 0, 0, 0))],
        out_specs=pl.BlockSpec((tq, gw), lambda h, i: (i, h)),
        out_shape=jax.ShapeDtypeStruct((s, n_heads * HEAD_DIM), BF16),
        scratch_shapes=[pltpu.VMEM((gh, tq, HEAD_DIM), F32), pltpu.VMEM((gh, tq, HEAD_DIM), F32),
                        pltpu.VMEM((gh, tq, HEAD_DIM), F32)],
        compiler_params=_params(("arbitrary", "arbitrary"), vmem),
        name="fox_attention",
    )(qkv, qkv, qkv, cum, ck)


POOL_HALO = 16


def _pool_proj_body(a_ref, w_ref, pw_ref, sc_ref, o_ref, halo_ref, *, tm):
    i = pl.program_id(0)
    g = pl.program_id(1)

    @pl.when(i == 0)
    def _():
        halo_ref[g] = jnp.zeros(halo_ref.shape[1:], halo_ref.dtype)

    z = jnp.dot(a_ref[...], w_ref[...], preferred_element_type=F32)
    ext = jnp.concatenate([halo_ref[g], z], axis=0)
    halo_ref[g] = z[-POOL_HALO:, :]
    t = (i * tm + lax.broadcasted_iota(jnp.int32, (tm, 1), 0) + 1).astype(F32)

    for gi, wnd in enumerate(POOL_WINDOWS):
        @pl.when(g == gi)
        def _(wnd=wnd):
            win = ext
            d = 1
            while d < wnd:
                win = win + pltpu.roll(win, d, axis=0)
                d *= 2
            pooled = win[POOL_HALO:] / jnp.minimum(t, float(wnd)) - z
            y = jnp.dot(pooled.astype(BF16), pw_ref[0], preferred_element_type=F32) * sc_ref[...]
            o_ref[...] = y.astype(o_ref.dtype)


def pool_proj(h, w_zp, pool_w, pool_scale, *, tm=1024):
    s, kdim = h.shape
    ng, gd, _ = pool_w.shape
    assert ng == len(POOL_WINDOWS) and max(POOL_WINDOWS) <= POOL_HALO
    width = ng * gd
    body = functools.partial(_pool_proj_body, tm=tm)
    vmem = (2 * tm * kdim * 2 + 2 * kdim * gd * 2 + 2 * gd * gd * 2 + 2 * tm * gd * 2
            + 10 * (tm + POOL_HALO) * gd * 4)
    return pl.pallas_call(
        body,
        grid=(s // tm, ng),
        in_specs=[pl.BlockSpec((tm, kdim), lambda i, g: (i, 0)),
                  pl.BlockSpec((kdim, gd), lambda i, g: (0, g)),
                  pl.BlockSpec((1, gd, gd), lambda i, g: (g, 0, 0)),
                  pl.BlockSpec((1, gd), lambda i, g: (0, g))],
        out_specs=pl.BlockSpec((tm, gd), lambda i, g: (i, g)),
        out_shape=jax.ShapeDtypeStruct((s, width), BF16),
        scratch_shapes=[pltpu.VMEM((ng, POOL_HALO, gd), F32)],
        compiler_params=_params(("arbitrary", "arbitrary"), vmem),
        name="pool_proj",
    )(h, w_zp, pool_w, pool_scale.reshape(1, width))


def _out_body(a1_ref, a2_ref, w1_ref, w2_ref, r_ref, o_ref):
    d = jnp.dot(a1_ref[...], w1_ref[...].astype(BF16), preferred_element_type=F32)
    d = d + jnp.dot(a2_ref[...], w2_ref[...].astype(BF16), preferred_element_type=F32)
    o_ref[...] = r_ref[...] + d


def out_proj_residual(a1, a2, w, res, *, tm=1024, tn=512):
    s, k1 = a1.shape
    k2 = a2.shape[1]
    assert k1 == k2
    n = w.shape[1]
    vmem = (4 * tm * k1 * 2 + 4 * k1 * tn * 4 + 2 * k1 * tn * 2 + 4 * tm * tn * 4
            + 3 * tm * tn * 4)
    return pl.pallas_call(
        _out_body,
        grid=(s // tm, n // tn),
        in_specs=[pl.BlockSpec((tm, k1), lambda i, j: (i, 0)),
                  pl.BlockSpec((tm, k2), lambda i, j: (i, 0)),
                  pl.BlockSpec((k1, tn), lambda i, j: (0, j)),
                  pl.BlockSpec((k2, tn), lambda i, j: (1, j)),
                  pl.BlockSpec((tm, tn), lambda i, j: (i, j))],
        out_specs=pl.BlockSpec((tm, tn), lambda i, j: (i, j)),
        out_shape=jax.ShapeDtypeStruct((s, n), F32),
        compiler_params=_params(("arbitrary", "arbitrary"), vmem),
        name="out_proj",
    )(a1, a2, w, w, res)


def _causal_conv3(x, prev, w_ref):
    ext = jnp.concatenate([prev, x], axis=0)
    x1 = pltpu.roll(ext, 1, axis=0)[SUBLANES:]
    x2 = pltpu.roll(ext, 2, axis=0)[SUBLANES:]
    return w_ref[0:1, :] * x2 + w_ref[1:2, :] * x1 + w_ref[2:3, :] * x


def _zero_halo_at_first_row_block(halo_ref, j):
    @pl.when(pl.program_id(0) == 0)
    def _():
        halo_ref[j] = jnp.zeros(halo_ref.shape[1:], halo_ref.dtype)


def _ffn_up_body(a_ref, wg_ref, wu_ref, cw_ref, o_ref, halo_ref, *, row_chunks):
    j = pl.program_id(1)
    _zero_halo_at_first_row_block(halo_ref, j)
    tn = wg_ref.shape[1]
    w = jnp.concatenate([wg_ref[...].astype(BF16), wu_ref[...].astype(BF16)], axis=1)
    tc = a_ref.shape[0] // row_chunks
    prev = halo_ref[j]
    for rc in range(row_chunks):
        rows = slice(rc * tc, (rc + 1) * tc)
        gu = jnp.dot(a_ref[rows, :], w, preferred_element_type=F32)
        g = gu[:, :tn]
        u = gu[:, tn:]
        c = _causal_conv3(g, prev, cw_ref)
        prev = g[-SUBLANES:, :]
        o_ref[rows, :] = ((c * jax.nn.sigmoid(c)) * u).astype(o_ref.dtype)
    halo_ref[j] = prev


def ffn_up(h, w_up, conv_w, *, tm=2048, tn=256, row_chunks=2):
    s, kdim = h.shape
    d_ff = w_up.shape[1] // 2
    nj = d_ff // tn
    tc = tm // row_chunks
    vmem = (tm * kdim * 2 + 4 * kdim * tn * 4 + 2 * kdim * tn * 2 + 2 * tm * tn * 2
            + 10 * tc * tn * 4)
    return pl.pallas_call(
        functools.partial(_ffn_up_body, row_chunks=row_chunks),
        grid=(s // tm, nj),
        in_specs=[_single((tm, kdim), lambda i, j: (i, 0)),
                  pl.BlockSpec((kdim, tn), lambda i, j: (0, j)),
                  pl.BlockSpec((kdim, tn), lambda i, j: (0, nj + j)),
                  pl.BlockSpec((3, tn), lambda i, j: (0, j))],
        out_specs=pl.BlockSpec((tm, tn), lambda i, j: (i, j)),
        out_shape=jax.ShapeDtypeStruct((s, d_ff), BF16),
        scratch_shapes=[pltpu.VMEM((nj, SUBLANES, tn), F32)],
        compiler_params=_params(("arbitrary", "arbitrary"), vmem),
        name="ffn_up",
    )(h, w_up, w_up, conv_w)


def _down_body(a_ref, w_ref, r_ref, o_ref):
    o_ref[...] = r_ref[...] + jnp.dot(a_ref[...], w_ref[...], preferred_element_type=F32)


def ffn_down_residual(act, w_down, res, *, tm=512, tn=512):
    s, kdim = act.shape
    n = w_down.shape[1]
    vmem = 2 * tm * kdim * 2 + 2 * kdim * tn * 2 + 4 * tm * tn * 4 + 2 * tm * tn * 4
    return pl.pallas_call(
        _down_body,
        grid=(s // tm, n // tn),
        in_specs=[pl.BlockSpec((tm, kdim), lambda i, j: (i, 0)),
                  pl.BlockSpec((kdim, tn), lambda i, j: (0, j)),
                  pl.BlockSpec((tm, tn), lambda i, j: (i, j))],
        out_specs=pl.BlockSpec((tm, tn), lambda i, j: (i, j)),
        out_shape=jax.ShapeDtypeStruct((s, n), F32),
        compiler_params=_params(("arbitrary", "arbitrary"), vmem),
        name="ffn_down",
    )(act, w_down, res)


def _gconv_body(a_ref, wv_ref, wb_ref, wc_ref, cw_ref, o_ref, halo_ref, *, row_chunks):
    j = pl.program_id(1)
    _zero_halo_at_first_row_block(halo_ref, j)
    tn = wv_ref.shape[1]
    w = jnp.concatenate([wv_ref[...].astype(BF16), wb_ref[...].astype(BF16),
                         wc_ref[...].astype(BF16)], axis=1)
    tc = a_ref.shape[0] // row_chunks
    prev = halo_ref[j]
    for rc in range(row_chunks):
        rows = slice(rc * tc, (rc + 1) * tc)
        vbc = jnp.dot(a_ref[rows, :], w, preferred_element_type=F32)
        v = vbc[:, :tn]
        gate_b = vbc[:, tn:2 * tn]
        gate_c = vbc[:, 2 * tn:]
        x = gate_c * v
        c = _causal_conv3(x, prev, cw_ref)
        prev = x[-SUBLANES:, :]
        o_ref[rows, :] = (gate_b * c).astype(o_ref.dtype)
    halo_ref[j] = prev


def gated_conv_proj(h, w_in, conv_w, *, tm=1024, tn=256, row_chunks=2):
    s, kdim = h.shape
    width = conv_w.shape[1]
    nj = width // tn
    tc = tm // row_chunks
    vmem = (tm * kdim * 2 + 6 * kdim * tn * 4 + 3 * kdim * tn * 2 + 2 * tm * tn * 2
            + 12 * tc * tn * 4)
    return pl.pallas_call(
        functools.partial(_gconv_body, row_chunks=row_chunks),
        grid=(s // tm, nj),
        in_specs=[_single((tm, kdim), lambda i, j: (i, 0)),
                  pl.BlockSpec((kdim, tn), lambda i, j: (0, j)),
                  pl.BlockSpec((kdim, tn), lambda i, j: (0, nj + j)),
                  pl.BlockSpec((kdim, tn), lambda i, j: (0, 2 * nj + j)),
                  pl.BlockSpec((3, tn), lambda i, j: (0, j))],
        out_specs=pl.BlockSpec((tm, tn), lambda i, j: (i, j)),
        out_shape=jax.ShapeDtypeStruct((s, width), BF16),
        scratch_shapes=[pltpu.VMEM((nj, SUBLANES, tn), F32)],
        compiler_params=_params(("arbitrary", "arbitrary"), vmem),
        name="gated_conv",
    )(h, w_in, w_in, w_in, conv_w)


SQRT_HALF = math.sqrt(0.5)


def _gelu_mm_body(a_ref, w_ref, o_ref):
    z = jnp.dot(a_ref[...], w_ref[...].astype(BF16), preferred_element_type=F32)
    o_ref[...] = 0.5 * z * (1.0 + lax.erf(z * SQRT_HALF))


def gelu_proj(h, w_in, *, col_block_offset, n_out, tm=1024, tn=512):
    s, kdim = h.shape
    vmem = tm * kdim * 2 + 2 * kdim * tn * 4 + kdim * tn * 2 + 2 * tm * tn * 4 + 6 * tm * tn * 4
    return pl.pallas_call(
        _gelu_mm_body,
        grid=(s // tm, n_out // tn),
        in_specs=[_single((tm, kdim), lambda i, j: (i, 0)),
                  pl.BlockSpec((kdim, tn), lambda i, j: (0, j + col_block_offset))],
        out_specs=pl.BlockSpec((tm, tn), lambda i, j: (i, j)),
        out_shape=jax.ShapeDtypeStruct((s, n_out), F32),
        compiler_params=_params(("arbitrary", "arbitrary"), vmem),
        name="gelu_proj",
    )(h, w_in)


def _sgu_body(u_ref, v_ref, g_ref, w_ref, bt_ref, o_ref, *, ts, n_heads):
    v = v_ref[...]
    ms = jnp.mean(v * v, axis=-1, keepdims=True)
    vn = ((v * lax.rsqrt(ms + EPS)) * g_ref[...]).astype(BF16)
    r = lax.broadcasted_iota(jnp.int32, (SGU_CHUNK, SGU_CHUNK), 0)
    c = lax.broadcasted_iota(jnp.int32, (SGU_CHUNK, SGU_CHUNK), 1)
    for h in range(n_heads):
        cols = slice(h * HEAD_DIM, (h + 1) * HEAD_DIM)
        w_tril = jnp.where(r >= c, w_ref[h], 0.0).astype(BF16)
        bias = bt_ref[:, h:h + 1]
        for n in range(ts // SGU_CHUNK):
            rows = slice(n * SGU_CHUNK, (n + 1) * SGU_CHUNK)
            mixed = jnp.dot(w_tril, vn[rows, cols], preferred_element_type=F32) + bias
            o_ref[rows, cols] = (u_ref[rows, cols] * mixed).astype(o_ref.dtype)


def spatial_gating(z, norm_g, w_s, b_s, *, ts=256):
    s, two_w = z.shape
    width = two_w // 2
    n_heads = w_s.shape[0]
    body = functools.partial(_sgu_body, ts=ts, n_heads=n_heads)
    vmem = 4 * ts * width * 4 + 2 * ts * width * 2 + 2 * n_heads * SGU_CHUNK * SGU_CHUNK * 4 + 6 * ts * width * 4
    return pl.pallas_call(
        body,
        grid=(s // ts,),
        in_specs=[pl.BlockSpec((ts, width), lambda i: (i, 0)),
                  pl.BlockSpec((ts, width), lambda i: (i, 1)),
                  pl.BlockSpec((1, width), lambda i: (0, 0)),
                  pl.BlockSpec((n_heads, SGU_CHUNK, SGU_CHUNK), lambda i: (0, 0, 0)),
                  pl.BlockSpec((SGU_CHUNK, n_heads), lambda i: (0, 0))],
        out_specs=pl.BlockSpec((ts, width), lambda i: (i, 0)),
        out_shape=jax.ShapeDtypeStruct((s, width), BF16),
        compiler_params=_params(("arbitrary",), vmem),
        name="spatial_gating",
    )(z, z, norm_g.reshape(1, width), w_s, b_s.T)


def _conv_glu_ffn(x, f_norm, f_up, f_conv, f_down):
    hn = rmsnorm_rows(x, f_norm)
    act = ffn_up(hn, f_up, f_conv)
    return ffn_down_residual(act, f_down.astype(BF16), x)


def kernel(x, l0_mix_norm, l0_w_in, l0_b_f, l0_q_gain, l0_k_gain, l0_pool_w, l0_pool_scale, l0_w_out, l0_ffn_norm, l0_ffn_w_up, l0_ffn_conv, l0_ffn_w_down, l1_mix_norm, l1_w_in, l1_conv_w, l1_sgu_norm, l1_sgu_w, l1_sgu_b, l1_w_out, l1_ffn_norm, l1_ffn_w_up, l1_ffn_conv, l1_ffn_w_down):
    b, s, d = x.shape
    assert b == 1
    x0 = x.reshape(s, d)
    half = d // 2
    n_heads = half // HEAD_DIM
    lanes = 128

    w_qkv = l0_w_in[:, :3 * half].astype(BF16)
    w_f = jnp.pad(l0_w_in[:, 3 * half:3 * half + n_heads].astype(BF16), ((0, 0), (0, lanes - n_heads)))
    w_zp = l0_w_in[:, 3 * half + n_heads:].astype(BF16)
    b_f = jnp.pad(l0_b_f, (0, lanes - n_heads)).reshape(1, lanes)
    qk_gain = jnp.concatenate([jnp.tile(l0_q_gain, n_heads), jnp.tile(l0_k_gain, n_heads)]).reshape(1, 2 * half)

    h0 = rmsnorm_rows(x0, l0_mix_norm)
    qkv = qkv_proj(h0, w_qkv, qk_gain)
    cum, cum_t = forget_gate(h0, w_f, b_f, n_heads)
    attn = fox_attention(qkv, cum, cum_t, n_heads)
    pm = pool_proj(h0, w_zp, l0_pool_w.astype(BF16), l0_pool_scale)
    x1 = out_proj_residual(attn, pm, l0_w_out, x0)
    x2 = _conv_glu_ffn(x1, l0_ffn_norm, l0_ffn_w_up, l0_ffn_conv, l0_ffn_w_down)

    h1 = rmsnorm_rows(x2, l1_mix_norm)
    c_out = gated_conv_proj(h1, l1_w_in, l1_conv_w)
    z = gelu_proj(h1, l1_w_in, col_block_offset=(3 * half) // 512, n_out=2 * half)
    d_out = spatial_gating(z, l1_sgu_norm, l1_sgu_w, l1_sgu_b)
    x3 = out_proj_residual(c_out, d_out, l1_w_out, x2)
    x4 = _conv_glu_ffn(x3, l1_ffn_norm, l1_ffn_w_up, l1_ffn_conv, l1_ffn_w_down)
    return x4.reshape(b, s, d)
```
